```python
import jax, jax.numpy as jnp
from jax import lax
import numpy as np

D_MODEL = 1024
BATCH = 8
SEQ = 8192
DEPTH = 4

GRID_W = 64
CTX_LEN = 256
N_MIXERS = 3
N_A = len(range(0, DEPTH, N_MIXERS))
N_B = len(range(1, DEPTH, N_MIXERS))
N_C = len(range(2, DEPTH, N_MIXERS))
GMLP_CHUNK = 128
GMLP_HALF = 2 * D_MODEL
GMLP_GROUPS = 8
DIFF_HEADS = 8
DIFF_HEAD_DIM = 64
DIFF_VALUE_DIM = 2 * DIFF_HEAD_DIM
DIFF_SCALE = DIFF_HEAD_DIM ** -0.5
ROPE_BASE = 10000.0
Q_BLOCK = 128
CONV_WIDTH = 3
N_EXPERTS = 32
TOP_K = 4
EXPERT_FF = D_MODEL
SWIGLU_LIMIT = 7.0
SWIGLU_ALPHA = 1.702
MOE_BLOCK = 256
DEEPNORM_ALPHA = (2 * DEPTH) ** 0.25
DEEPNORM_BETA = (8 * DEPTH) ** -0.25
LN_EPS = 1e-5

kernel_name = 'hybrid_interleaved_gmlp_diffattn_shortconv_moe_dit'


def layer_norm(x, g, b):
    xf = x.astype(jnp.float32)
    mu = jnp.mean(xf, -1, keepdims=True)
    var = jnp.mean(jnp.square(xf - mu), -1, keepdims=True)
    return ((xf - mu) * lax.rsqrt(var + LN_EPS) * g + b).astype(x.dtype)


def rms_norm(x, g):
    xf = x.astype(jnp.float32)
    ms = jnp.mean(jnp.square(xf), -1, keepdims=True)
    return (xf * lax.rsqrt(ms + LN_EPS) * g).astype(x.dtype)


def modulate(x, shift, scale):
    return x * (1 + scale) + shift


def chunk_gmlp(h, w_in, ln_g, ln_b, w_s, b_s, w_out):
    B, L, _ = h.shape
    z = jax.nn.gelu(h @ w_in, approximate=False)
    u, v = z[..., :GMLP_HALF], z[..., GMLP_HALF:]
    v = layer_norm(v, ln_g, ln_b)
    v = v.reshape(B, L // GMLP_CHUNK, GMLP_CHUNK, GMLP_GROUPS, GMLP_HALF // GMLP_GROUPS)
    v = jnp.einsum('gpq,bnqge->bnpge', w_s.astype(v.dtype), v) + b_s.T[None, None, :, :, None]
    return (u * v.reshape(B, L, GMLP_HALF)) @ w_out


def rope_1d(x, pos):
    n = x.shape[-1] // 2
    inv = ROPE_BASE ** (-jnp.arange(n, dtype=jnp.float32) / n)
    ang = pos.astype(jnp.float32)[:, None] * inv
    shape = (1, pos.shape[0]) + (1,) * (x.ndim - 3) + (n,)
    cos = jnp.cos(ang).reshape(shape).astype(x.dtype)
    sin = jnp.sin(ang).reshape(shape).astype(x.dtype)
    x1, x2 = x[..., :n], x[..., n:]
    return jnp.concatenate([x1 * cos - x2 * sin, x1 * sin + x2 * cos], -1)


def axial_rope(x, rows, cols):
    half = x.shape[-1] // 2
    return jnp.concatenate([rope_1d(x[..., :half], rows), rope_1d(x[..., half:], cols)], -1)


def diff_attention(h_l, h_c, w_qkv, lam_params, subln_g, w_o, lam_init, rows, cols, need_ctx):
    B, S, _ = h_l.shape
    qk_w = DIFF_HEADS * 2 * DIFF_HEAD_DIM

    def project(h):
        qkv = h @ w_qkv
        lead = h.shape[:2]
        q = qkv[..., :qk_w].reshape(lead + (DIFF_HEADS, 2, DIFF_HEAD_DIM))
        k = qkv[..., qk_w:2 * qk_w].reshape(lead + (DIFF_HEADS, 2, DIFF_HEAD_DIM))
        v = qkv[..., 2 * qk_w:].reshape(lead + (DIFF_HEADS, DIFF_VALUE_DIM))
        return q, k, v

    q_l, k_l, v_l = project(h_l)
    q_l = axial_rope(q_l, rows, cols)
    k_l = axial_rope(k_l, rows, cols)
    q_c, k_c, v_c = project(h_c)
    lp = lam_params.astype(jnp.float32)
    lam = jnp.exp(jnp.sum(lp[0] * lp[1])) - jnp.exp(jnp.sum(lp[2] * lp[3])) + lam_init
    k_all = jnp.concatenate([k_c, k_l], axis=1)
    v_all = jnp.concatenate([v_c, v_l], axis=1)

    def attend(q, k, v):
        s = jnp.einsum('bqhcd,bkhcd->bhcqk', q, k).astype(jnp.float32) * DIFF_SCALE
        p = jax.nn.softmax(s, axis=-1)
        a = p[:, :, 0] - lam * p[:, :, 1]
        return jnp.einsum('bhqk,bkhe->bqhe', a.astype(v.dtype), v)

    n_blk = S // Q_BLOCK
    qb = q_l.reshape(B, n_blk, Q_BLOCK, DIFF_HEADS, 2, DIFF_HEAD_DIM).swapaxes(0, 1)
    o_l = lax.map(lambda q: attend(q, k_all, v_all), qb)
    o_l = o_l.swapaxes(0, 1).reshape(B, S, DIFF_HEADS, DIFF_VALUE_DIM)

    def finish(o):
        o = rms_norm(o, subln_g) * (1.0 - lam_init)
        return o.reshape(o.shape[:2] + (DIFF_HEADS * DIFF_VALUE_DIM,)) @ w_o

    y_l = finish(o_l)
    y_c = finish(attend(q_c, k_c, v_c)) if need_ctx else None
    return y_c, y_l


def depthwise_conv(x, w):
    pad = CONV_WIDTH // 2
    return lax.conv_general_dilated(
        x, w[:, None, :].astype(x.dtype), window_strides=(1,), padding=((pad, pad),),
        dimension_numbers=('NWC', 'WIO', 'NWC'), feature_group_count=x.shape[-1])


def short_conv(h, w_in, conv_w, w_out):
    bch = h @ w_in
    b_g, c_g, hx = jnp.split(bch, 3, axis=-1)
    return (b_g * depthwise_conv(c_g * hx, conv_w)) @ w_out


def expert_swiglu(xb, w_gu, b_gu, w_dn, b_dn):
    gu = xb @ w_gu + b_gu
    gate = jnp.minimum(gu[..., :EXPERT_FF], SWIGLU_LIMIT)
    up = jnp.clip(gu[..., EXPERT_FF:], -SWIGLU_LIMIT, SWIGLU_LIMIT)
    glu = gate * jax.nn.sigmoid(gate * SWIGLU_ALPHA)
    return ((up + 1) * glu) @ w_dn + b_dn


def moe_ffn(h, w_router, b_router, w_gu, b_gu, w_dn, b_dn):
    T, D = h.shape
    logits = (h @ w_router).astype(jnp.float32) + b_router.astype(jnp.float32)
    top_logit, top_e = lax.top_k(logits, TOP_K)
    gates = jax.nn.softmax(top_logit, axis=-1)
    n_assign = T * TOP_K
    flat_e = top_e.reshape(-1)
    order = jnp.argsort(flat_e)
    sorted_e = flat_e[order]
    counts = jnp.bincount(flat_e, length=N_EXPERTS)
    padded = (counts + MOE_BLOCK - 1) // MOE_BLOCK * MOE_BLOCK
    pad_end = jnp.cumsum(padded)
    pad_start = pad_end - padded
    start = jnp.cumsum(counts) - counts
    dest = pad_start[sorted_e] + jnp.arange(n_assign) - start[sorted_e]
    n_blocks = -(-n_assign // MOE_BLOCK) + N_EXPERTS
    n_slots = n_blocks * MOE_BLOCK
    slot_token = jnp.full((n_slots,), T, jnp.int32).at[dest].set((order // TOP_K).astype(jnp.int32))
    slot_gate = jnp.zeros((n_slots,), jnp.float32).at[dest].set(gates.reshape(-1)[order])
    block_e = jnp.minimum(
        jnp.searchsorted(pad_end, jnp.arange(n_blocks) * MOE_BLOCK, side='right'), N_EXPERTS - 1)
    h_pad = jnp.concatenate([h, jnp.zeros((1, D), h.dtype)], axis=0)

    def run_block(args):
        tok, g, e = args
        y = expert_swiglu(h_pad[tok], w_gu[e], b_gu[e], w_dn[e], b_dn[e])
        return y * g[:, None].astype(y.dtype)

    y = lax.map(run_block, (slot_token.reshape(n_blocks, MOE_BLOCK),
                            slot_gate.reshape(n_blocks, MOE_BLOCK), block_e))
    out = jax.ops.segment_sum(y.reshape(n_slots, D), slot_token, num_segments=T + 1)
    return out[:T]


def setup_inputs(seed: int = 0) -> dict:
    key = jax.random.key(seed)
    ks = iter(jax.random.split(key, 32))
    D = D_MODEL
    f32 = jnp.float32

    def nrm(shape, scale):
        return jax.random.normal(next(ks), shape, f32) * scale

    qkv_w = 2 * (DIFF_HEADS * 2 * DIFF_HEAD_DIM) + DIFF_HEADS * DIFF_VALUE_DIM
    return {
        'x': nrm((BATCH, SEQ, D), 1.0),
        'c': nrm((BATCH, D), 1.0),
        'ctx': nrm((BATCH, CTX_LEN, D), 1.0),
        'c_ctx': nrm((D,), 1.0),
        'w_mod': nrm((DEPTH, D, 6 * D), D ** -0.5),
        'b_mod': nrm((DEPTH, 6 * D), 0.02),
        'ln_g': 1.0 + nrm((DEPTH, 2, D), 0.02),
        'ln_b': nrm((DEPTH, 2, D), 0.02),
        'gmlp_w_in': nrm((N_A, D, 2 * GMLP_HALF), D ** -0.5),
        'gmlp_ln_g': 1.0 + nrm((N_A, GMLP_HALF), 0.02),
        'gmlp_ln_b': nrm((N_A, GMLP_HALF), 0.02),
        'gmlp_w_s': nrm((N_A, GMLP_GROUPS, GMLP_CHUNK, GMLP_CHUNK), GMLP_CHUNK ** -0.5),
        'gmlp_b_s': 1.0 + nrm((N_A, GMLP_GROUPS, GMLP_CHUNK), 0.02),
        'gmlp_w_out': nrm((N_A, GMLP_HALF, D), GMLP_HALF ** -0.5 * DEEPNORM_BETA),
        'diff_w_qkv': nrm((N_B, D, qkv_w), D ** -0.5),
        'diff_lambda': nrm((N_B, 4, DIFF_HEAD_DIM), 0.1),
        'diff_subln_g': 1.0 + nrm((N_B, DIFF_VALUE_DIM), 0.02),
        'diff_w_o': nrm((N_B, DIFF_HEADS * DIFF_VALUE_DIM, D),
                        (DIFF_HEADS * DIFF_VALUE_DIM) ** -0.5 * DEEPNORM_BETA),
        'conv_w_in': nrm((N_C, D, 3 * D), D ** -0.5),
        'conv_w': nrm((N_C, CONV_WIDTH, D), CONV_WIDTH ** -0.5),
        'conv_w_out': nrm((N_C, D, D), D ** -0.5 * DEEPNORM_BETA),
        'moe_w_router': nrm((DEPTH, D, N_EXPERTS), D ** -0.5),
        'moe_b_router': nrm((DEPTH, N_EXPERTS), 0.01),
        'moe_w_gate_up': nrm((DEPTH, N_EXPERTS, D, 2 * EXPERT_FF), D ** -0.5),
        'moe_b_gate_up': nrm((DEPTH, N_EXPERTS, 2 * EXPERT_FF), 0.02),
        'moe_w_down': nrm((DEPTH, N_EXPERTS, EXPERT_FF, D), EXPERT_FF ** -0.5 * DEEPNORM_BETA),
        'moe_b_down': nrm((DEPTH, N_EXPERTS, D), 0.02),
    }


def reference(x, c, ctx, c_ctx, w_mod, b_mod, ln_g, ln_b,
              gmlp_w_in, gmlp_ln_g, gmlp_ln_b, gmlp_w_s, gmlp_b_s, gmlp_w_out,
              diff_w_qkv, diff_lambda, diff_subln_g, diff_w_o,
              conv_w_in, conv_w, conv_w_out,
              moe_w_router, moe_b_router, moe_w_gate_up, moe_b_gate_up, moe_w_down, moe_b_down):
    D = D_MODEL
    S = x.shape[1]
    t = jnp.arange(S)
    rows, cols = t // GRID_W, t % GRID_W
    cond_l = jax.nn.silu(c)
    cond_c = jax.nn.silu(c_ctx)
    for i in range(DEPTH):
        last = i == DEPTH - 1
        kind = i % N_MIXERS
        j = i // N_MIXERS
        mod_l = jnp.split((cond_l @ w_mod[i] + b_mod[i])[:, None, :], 6, axis=-1)
        mod_c = jnp.split((cond_c @ w_mod[i] + b_mod[i])[None, None, :], 6, axis=-1)
        h_l = modulate(x, mod_l[0], mod_l[1])
        h_c = modulate(ctx, mod_c[0], mod_c[1]) if (kind == 1 or not last) else None
        if kind == 0:
            y_l = chunk_gmlp(h_l, gmlp_w_in[j], gmlp_ln_g[j], gmlp_ln_b[j], gmlp_w_s[j], gmlp_b_s[j], gmlp_w_out[j])
            y_c = None if last else chunk_gmlp(h_c, gmlp_w_in[j], gmlp_ln_g[j], gmlp_ln_b[j],
                                               gmlp_w_s[j], gmlp_b_s[j], gmlp_w_out[j])
        elif kind == 1:
            lam_init = 0.8 - 0.6 * float(np.exp(-0.3 * i))
            y_c, y_l = diff_attention(h_l, h_c, diff_w_qkv[j], diff_lambda[j], diff_subln_g[j], diff_w_o[j],
                                      lam_init, rows, cols, not last)
        else:
            y_l = short_conv(h_l, conv_w_in[j], conv_w[j], conv_w_out[j])
            y_c = None if last else short_conv(h_c, conv_w_in[j], conv_w[j], conv_w_out[j])
        x = layer_norm(DEEPNORM_ALPHA * x + mod_l[2] * y_l, ln_g[i, 0], ln_b[i, 0])
        if not last:
            ctx = layer_norm(DEEPNORM_ALPHA * ctx + mod_c[2] * y_c, ln_g[i, 0], ln_b[i, 0])
        h_l = modulate(x, mod_l[3], mod_l[4])
        moe_args = (moe_w_router[i], moe_b_router[i], moe_w_gate_up[i], moe_b_gate_up[i],
                    moe_w_down[i], moe_b_down[i])
        if last:
            y_l = moe_ffn(h_l.reshape(-1, D), *moe_args).reshape(h_l.shape)
        else:
            h_c = modulate(ctx, mod_c[3], mod_c[4])
            n_ctx = h_c.shape[0] * h_c.shape[1]
            y = moe_ffn(jnp.concatenate([h_c.reshape(-1, D), h_l.reshape(-1, D)], axis=0), *moe_args)
            y_c = y[:n_ctx].reshape(h_c.shape)
            y_l = y[n_ctx:].reshape(h_l.shape)
            ctx = layer_norm(DEEPNORM_ALPHA * ctx + mod_c[5] * y_c, ln_g[i, 1], ln_b[i, 1])
        x = layer_norm(DEEPNORM_ALPHA * x + mod_l[5] * y_l, ln_g[i, 1], ln_b[i, 1])
    return x
```

```python
import functools

import numpy as np
import jax
import jax.numpy as jnp
from jax import lax
from jax.experimental import pallas as pl
from jax.experimental.pallas import tpu as pltpu

GRID_W = 64
N_MIXERS = 3
GMLP_CHUNK = 128
DIFF_HEADS = 8
DIFF_HEAD_DIM = 64
DIFF_VALUE_DIM = 2 * DIFF_HEAD_DIM
DIFF_SCALE = DIFF_HEAD_DIM ** -0.5
ROPE_BASE = 10000.0
TOP_K = 4
SWIGLU_LIMIT = 7.0
SWIGLU_ALPHA = 1.702
LN_EPS = 1e-5

TM = 256
LANES = 128
MOD_ROWS = 16
VMEM_LIMIT = 56 * 1024 * 1024

F32 = jnp.float32
BF16 = jnp.bfloat16
NT_DIMS = (((1,), (1,)), ((), ()))


def _params(*sem):
    return pltpu.CompilerParams(dimension_semantics=sem, vmem_limit_bytes=VMEM_LIMIT)


def _layer_norm(r, g, b):
    mu = jnp.mean(r, axis=-1, keepdims=True)
    d = r - mu
    var = jnp.mean(d * d, axis=-1, keepdims=True)
    return d * lax.rsqrt(var + LN_EPS) * g + b


def _resid_ln(alpha, x, y, gate, g, b):
    return _layer_norm(alpha * x + gate * y, g, b)


def _group_map(ntb, nlt):
    def group(i):
        return jnp.where((i % ntb) >= nlt, 0, 1 + i // ntb)
    return group


def _mod_kernel(c_ref, w_ref, b_ref, o_ref):
    n = pl.program_id(1)
    c = c_ref[...]
    s = c * jax.nn.sigmoid(c)
    y = jnp.dot(s, w_ref[...], precision=lax.Precision.HIGHEST, preferred_element_type=F32)
    o_ref[...] = y + b_ref[...] + jnp.where((n % 3) == 1, 1.0, 0.0)


def _modulation(cond, w_mod, b_mod):
    depth, d, _ = w_mod.shape
    return pl.pallas_call(
        _mod_kernel,
        grid=(depth, 6),
        in_specs=[
            pl.BlockSpec((MOD_ROWS, d), lambda l, n: (0, 0)),
            pl.BlockSpec((None, d, d), lambda l, n: (l, 0, n)),
            pl.BlockSpec((None, 1, d), lambda l, n: (l, 0, n)),
        ],
        out_specs=pl.BlockSpec((None, MOD_ROWS, d), lambda l, n: (l, 0, n)),
        out_shape=jax.ShapeDtypeStruct((depth, MOD_ROWS, 6 * d), F32),
        compiler_params=_params("parallel", "parallel"),
        name="modulation",
    )(cond, w_mod, b_mod.reshape(depth, 1, 6 * d))


def _gmlp_kernel(alpha, col_chunk, x_ref, mod_ref, win_ref, vg_ref, vb_ref, ws_ref, bst_ref,
                 wout_ref, lg_ref, lb_ref, o_ref, u_scr, v_scr, p_scr):
    half = u_scr.shape[1]
    groups = ws_ref.shape[0]
    ge = half // groups
    x = x_ref[...]
    hb = (x * mod_ref[1:2, :] + mod_ref[0:1, :]).astype(BF16)
    for j in range(2 * half // col_chunk):
        z = jnp.dot(hb, win_ref[:, j * col_chunk:(j + 1) * col_chunk], preferred_element_type=F32)
        z = 0.5 * z * (1.0 + lax.erf(z * 0.7071067811865476))
        c0 = j * col_chunk
        if c0 < half:
            u_scr[:, c0:c0 + col_chunk] = z.astype(BF16)
        else:
            v_scr[:, c0 - half:c0 - half + col_chunk] = z
    vn = _layer_norm(v_scr[...], vg_ref[...], vb_ref[...]).astype(BF16)
    for g in range(groups):
        w = ws_ref[g]
        bias = bst_ref[:, g:g + 1]
        for c in range(TM // GMLP_CHUNK):
            r0 = c * GMLP_CHUNK
            mixed = jnp.dot(w, vn[r0:r0 + GMLP_CHUNK, g * ge:(g + 1) * ge],
                            preferred_element_type=F32) + bias
            u = u_scr[r0:r0 + GMLP_CHUNK, g * ge:(g + 1) * ge].astype(F32)
            p_scr[r0:r0 + GMLP_CHUNK, g * ge:(g + 1) * ge] = (u * mixed).astype(BF16)
    y = jnp.dot(p_scr[...], wout_ref[...], preferred_element_type=F32)
    o_ref[...] = _resid_ln(alpha, x, y, mod_ref[2:3, :], lg_ref[...], lb_ref[...])


def _gmlp_layer(xs, mod, group, alpha, w_in, vg, vb, w_s, b_s, w_out, lg, lb):
    t, d = xs.shape
    half = w_out.shape[0]
    groups = w_s.shape[0]
    col_chunk = min(512, half)
    const2 = lambda i: (0, 0)
    return pl.pallas_call(
        functools.partial(_gmlp_kernel, alpha, col_chunk),
        grid=(t // TM,),
        in_specs=[
            pl.BlockSpec((TM, d), lambda i: (i, 0)),
            pl.BlockSpec((None, 6, d), lambda i: (group(i), 0, 0)),
            pl.BlockSpec((d, 2 * half), const2),
            pl.BlockSpec((1, half), const2),
            pl.BlockSpec((1, half), const2),
            pl.BlockSpec((groups, GMLP_CHUNK, GMLP_CHUNK), lambda i: (0, 0, 0)),
            pl.BlockSpec((GMLP_CHUNK, groups), const2),
            pl.BlockSpec((half, d), const2),
            pl.BlockSpec((1, d), const2),
            pl.BlockSpec((1, d), const2),
        ],
        out_specs=pl.BlockSpec((TM, d), lambda i: (i, 0)),
        out_shape=jax.ShapeDtypeStruct((t, d), F32),
        scratch_shapes=[pltpu.VMEM((TM, half), BF16), pltpu.VMEM((TM, half), F32),
                        pltpu.VMEM((TM, half), BF16)],
        compiler_params=_params("parallel"),
        name="gmlp_mixer",
    )(xs, mod, w_in.astype(BF16), vg.reshape(1, half), vb.reshape(1, half), w_s.astype(BF16),
      b_s.T, w_out.astype(BF16), lg.reshape(1, d), lb.reshape(1, d))


def _rope_tables(seq):
    n = DIFF_HEAD_DIM // 4
    inv = ROPE_BASE ** (-np.arange(n, dtype=np.float64) / n)
    t = np.arange(seq)
    rows, cols = t // GRID_W, t % GRID_W
    lane = np.arange(LANES) % DIFF_HEAD_DIM
    quarter = lane // n
    pos = np.where(quarter[None, :] < 2, rows[:, None], cols[:, None]).astype(np.float64)
    ang = pos * inv[lane % n][None, :]
    first = (quarter % 2 == 0)[None, :]
    cos = np.cos(ang)
    sin_next = np.where(first, -np.sin(ang), 0.0)
    sin_prev = np.where(first, 0.0, np.sin(ang))
    ident = [np.ones((TM, LANES)), np.zeros((TM, LANES)), np.zeros((TM, LANES))]
    return [jnp.asarray(np.concatenate([a, i], 0), F32) for a, i in zip((cos, sin_next, sin_prev), ident)]


def _qkv_kernel(hw, x_ref, mod_ref, w_ref, cos_ref, sn_ref, sp_ref, q_ref, k_ref, v_ref):
    hb = (x_ref[...] * mod_ref[1:2, :] + mod_ref[0:1, :]).astype(BF16)
    cos, sn, sp = cos_ref[...], sn_ref[...], sp_ref[...]
    quarter = DIFF_HEAD_DIM // 4
    for part, o_ref in enumerate((q_ref, k_ref)):
        for j in range(hw // LANES):
            c0 = part * hw + j * LANES
            z = jnp.dot(hb, w_ref[:, c0:c0 + LANES], preferred_element_type=F32)
            z = (z * cos + pltpu.roll(z, LANES - quarter, 1) * sn + pltpu.roll(z, quarter, 1) * sp)
            o_ref[:, j * LANES:(j + 1) * LANES] = z.astype(BF16)
    for j in range(hw // LANES):
        c0 = 2 * hw + j * LANES
        v_ref[:, j * LANES:(j + 1) * LANES] = jnp.dot(
            hb, w_ref[:, c0:c0 + LANES], preferred_element_type=F32).astype(BF16)


def _flash_kernel(nlt, ctx_len, lam_init, q_ref, k_ref, v_ref, lp_ref, sg_ref, o_ref,
                  qq_scr, m_scr, l_scr, acc_scr):
    qi, ki, nk = pl.program_id(2), pl.program_id(3), pl.num_programs(3)
    tq = q_ref.shape[0]
    tk = k_ref.shape[0]

    @pl.when(ki == 0)
    def _():
        q = q_ref[...]
        lane = lax.broadcasted_iota(jnp.int32, q.shape, 1)
        zero = jnp.zeros_like(q)
        qq_scr[0:tq, :] = jnp.where(lane < DIFF_HEAD_DIM, q, zero)
        qq_scr[tq:2 * tq, :] = jnp.where(lane >= DIFF_HEAD_DIM, q, zero)
        m_scr[...] = jnp.full(m_scr.shape, -jnp.inf, F32)
        l_scr[...] = jnp.zeros(l_scr.shape, F32)
        acc_scr[...] = jnp.zeros(acc_scr.shape, F32)

    def step(context_only):
        s = lax.dot_general(qq_scr[...], k_ref[...], NT_DIMS, preferred_element_type=F32)
        if context_only:
            col = lax.broadcasted_iota(jnp.int32, s.shape, 1)
            s = jnp.where(col >= tk - ctx_len, s, -jnp.inf)
        m_old = m_scr[...]
        m_new = jnp.maximum(m_old, jnp.max(s, axis=-1, keepdims=True))
        a = jnp.exp(m_old - m_new)
        p = jnp.exp(s - m_new)
        l_scr[...] = a * l_scr[...] + jnp.sum(p, axis=-1, keepdims=True)
        acc_scr[...] = a * acc_scr[...] + jnp.dot(p.astype(BF16), v_ref[...], preferred_element_type=F32)
        m_scr[...] = m_new

    pl.when(qi < nlt)(lambda: step(False))
    pl.when(jnp.logical_and(qi >= nlt, ki == nk - 1))(lambda: step(True))

    @pl.when(ki == nk - 1)
    def _():
        lp = lp_ref[...]
        lam = (jnp.exp(jnp.sum(lp[0:1] * lp[1:2], axis=-1, keepdims=True))
               - jnp.exp(jnp.sum(lp[2:3] * lp[3:4], axis=-1, keepdims=True)) + lam_init)
        acc = acc_scr[...]
        l = l_scr[...]
        o = acc[0:tq] / l[0:tq] - lam * (acc[tq:2 * tq] / l[tq:2 * tq])
        ms = jnp.mean(o * o, axis=-1, keepdims=True)
        o_ref[...] = (o * lax.rsqrt(ms + LN_EPS) * sg_ref[...] * (1.0 - lam_init)).astype(BF16)


def _attn_out_kernel(alpha, x_ref, mod_ref, a_ref, w_ref, lg_ref, lb_ref, o_ref):
    y = jnp.dot(a_ref[...], w_ref[...], preferred_element_type=F32)
    o_ref[...] = _resid_ln(alpha, x_ref[...], y, mod_ref[2:3, :], lg_ref[...], lb_ref[...])


def _attn_layer(xs, mod, group, alpha, dims, lam_init, w_qkv, lam_params, subln_g, w_o, lg, lb):
    t, d = xs.shape
    nb, seq, ctx_len = dims
    per = seq + ctx_len
    ntb, nlt = per // TM, seq // TM
    hw = DIFF_HEADS * DIFF_VALUE_DIM
    const2 = lambda i: (0, 0)
    scale = jnp.concatenate([jnp.full((hw,), DIFF_SCALE, F32), jnp.ones((2 * hw,), F32)])
    w_qkv = (w_qkv * scale[None, :]).astype(BF16)
    table_block = lambda i: (jnp.where((i % ntb) >= nlt, nlt, i % ntb), 0)
    q, k, v = pl.pallas_call(
        functools.partial(_qkv_kernel, hw),
        grid=(t // TM,),
        in_specs=[
            pl.BlockSpec((TM, d), lambda i: (i, 0)),
            pl.BlockSpec((None, 6, d), lambda i: (group(i), 0, 0)),
            pl.BlockSpec((d, 3 * hw), const2),
            pl.BlockSpec((TM, LANES), table_block),
            pl.BlockSpec((TM, LANES), table_block),
            pl.BlockSpec((TM, LANES), table_block),
        ],
        out_specs=[pl.BlockSpec((TM, hw), lambda i: (i, 0))] * 3,
        out_shape=[jax.ShapeDtypeStruct((t, hw), BF16)] * 3,
        compiler_params=_params("parallel"),
        name="attn_qkv",
    )(xs, mod, w_qkv, *_rope_tables(seq))

    tk = 768 if (per % 768 == 0 and per > 768) else TM
    assert ctx_len <= tk and ctx_len % TM == 0 and per % tk == 0
    q, k, v = (a.reshape(nb, per, hw) for a in (q, k, v))
    att = pl.pallas_call(
        functools.partial(_flash_kernel, nlt, ctx_len, lam_init),
        grid=(nb, DIFF_HEADS, ntb, per // tk),
        in_specs=[
            pl.BlockSpec((None, TM, DIFF_VALUE_DIM), lambda b, h, qi, ki: (b, qi, h)),
            pl.BlockSpec((None, tk, DIFF_VALUE_DIM), lambda b, h, qi, ki: (b, ki, h)),
            pl.BlockSpec((None, tk, DIFF_VALUE_DIM), lambda b, h, qi, ki: (b, ki, h)),
            pl.BlockSpec((4, DIFF_HEAD_DIM), lambda b, h, qi, ki: (0, 0)),
            pl.BlockSpec((1, DIFF_VALUE_DIM), lambda b, h, qi, ki: (0, 0)),
        ],
        out_specs=pl.BlockSpec((None, TM, DIFF_VALUE_DIM), lambda b, h, qi, ki: (b, qi, h)),
        out_shape=jax.ShapeDtypeStruct((nb, per, hw), BF16),
        scratch_shapes=[pltpu.VMEM((2 * TM, DIFF_VALUE_DIM), BF16), pltpu.VMEM((2 * TM, 1), F32),
                        pltpu.VMEM((2 * TM, 1), F32), pltpu.VMEM((2 * TM, DIFF_VALUE_DIM), F32)],
        compiler_params=_params("parallel", "parallel", "parallel", "arbitrary"),
        name="attn_flash",
    )(q, k, v, lam_params.astype(F32), subln_g.reshape(1, DIFF_VALUE_DIM))

    return pl.pallas_call(
        functools.partial(_attn_out_kernel, alpha),
        grid=(t // TM,),
        in_specs=[
            pl.BlockSpec((TM, d), lambda i: (i, 0)),
            pl.BlockSpec((None, 6, d), lambda i: (group(i), 0, 0)),
            pl.BlockSpec((TM, hw), lambda i: (i, 0)),
            pl.BlockSpec((hw, d), const2),
            pl.BlockSpec((1, d), const2),
            pl.BlockSpec((1, d), const2),
        ],
        out_specs=pl.BlockSpec((TM, d), lambda i: (i, 0)),
        out_shape=jax.ShapeDtypeStruct((t, d), F32),
        compiler_params=_params("parallel"),
        name="attn_out",
    )(xs, mod, att.reshape(t, hw), w_o.astype(BF16), lg.reshape(1, d), lb.reshape(1, d))


HALO = 16


def _conv_in_kernel(d, x_ref, mod_ref, w_ref, bg_ref, u_ref):
    hb = (x_ref[...] * mod_ref[1:2, :] + mod_ref[0:1, :]).astype(BF16)
    chunk = min(512, d)
    for j in range(d // chunk):
        c0 = j * chunk
        bg = jnp.dot(hb, w_ref[:, c0:c0 + chunk], preferred_element_type=F32)
        cg = jnp.dot(hb, w_ref[:, d + c0:d + c0 + chunk], preferred_element_type=F32)
        hx = jnp.dot(hb, w_ref[:, 2 * d + c0:2 * d + c0 + chunk], preferred_element_type=F32)
        bg_ref[:, c0:c0 + chunk] = bg.astype(BF16)
        u_ref[:, c0:c0 + chunk] = (cg * hx).astype(BF16)


def _conv_out_kernel(alpha, ntb, nlt, x_ref, mod_ref, bg_ref, u_ref, up_ref, un_ref, cw_ref, w_ref,
                     lg_ref, lb_ref, o_ref):
    j = pl.program_id(0) % ntb
    keep_prev = jnp.where(jnp.logical_or(j == 0, j == nlt), 0.0, 1.0)
    keep_next = jnp.where(jnp.logical_or(j == nlt - 1, j == ntb - 1), 0.0, 1.0)
    u = u_ref[...].astype(F32)
    row = lax.broadcasted_iota(jnp.int32, u.shape, 0)
    prev_row = up_ref[HALO - 1:HALO, :].astype(F32) * keep_prev
    next_row = un_ref[0:1, :].astype(F32) * keep_next
    u_prev = jnp.where(row == 0, prev_row, pltpu.roll(u, 1, 0))
    u_next = jnp.where(row == TM - 1, next_row, pltpu.roll(u, TM - 1, 0))
    conv = cw_ref[0:1, :] * u_prev + cw_ref[1:2, :] * u + cw_ref[2:3, :] * u_next
    tb = (bg_ref[...].astype(F32) * conv).astype(BF16)
    y = jnp.dot(tb, w_ref[...], preferred_element_type=F32)
    o_ref[...] = _resid_ln(alpha, x_ref[...], y, mod_ref[2:3, :], lg_ref[...], lb_ref[...])


def _conv_layer(xs, mod, group, alpha, dims, w_in, conv_w, w_out, lg, lb):
    t, d = xs.shape
    nb, seq, ctx_len = dims
    ntb, nlt = (seq + ctx_len) // TM, seq // TM
    const2 = lambda i: (0, 0)
    bg, u = pl.pallas_call(
        functools.partial(_conv_in_kernel, d),
        grid=(t // TM,),
        in_specs=[
            pl.BlockSpec((TM, d), lambda i: (i, 0)),
            pl.BlockSpec((None, 6, d), lambda i: (group(i), 0, 0)),
            pl.BlockSpec((d, 3 * d), const2),
        ],
        out_specs=[pl.BlockSpec((TM, d), lambda i: (i, 0))] * 2,
        out_shape=[jax.ShapeDtypeStruct((t, d), BF16)] * 2,
        compiler_params=_params("parallel"),
        name="conv_in",
    )(xs, mod, w_in.astype(BF16))
    per_tile = TM // HALO
    last_halo = t // HALO - 1
    return pl.pallas_call(
        functools.partial(_conv_out_kernel, alpha, ntb, nlt),
        grid=(t // TM,),
        in_specs=[
            pl.BlockSpec((TM, d), lambda i: (i, 0)),
            pl.BlockSpec((None, 6, d), lambda i: (group(i), 0, 0)),
            pl.BlockSpec((TM, d), lambda i: (i, 0)),
            pl.BlockSpec((TM, d), lambda i: (i, 0)),
            pl.BlockSpec((HALO, d), lambda i: (jnp.maximum(i * per_tile - 1, 0), 0)),
            pl.BlockSpec((HALO, d), lambda i: (jnp.minimum((i + 1) * per_tile, last_halo), 0)),
            pl.BlockSpec((3, d), const2),
            pl.BlockSpec((d, d), const2),
            pl.BlockSpec((1, d), const2),
            pl.BlockSpec((1, d), const2),
        ],
        out_specs=pl.BlockSpec((TM, d), lambda i: (i, 0)),
        out_shape=jax.ShapeDtypeStruct((t, d), F32),
        compiler_params=_params("parallel"),
        name="conv_out",
    )(xs, mod, bg, u, u, u, conv_w, w_out.astype(BF16), lg.reshape(1, d), lb.reshape(1, d))


def _route_kernel(x_ref, mod_ref, wr_ref, br_ref, h_ref, e_ref, g_ref, r_ref, cnt_ref, carry):
    ne = wr_ref.shape[0]

    @pl.when(pl.program_id(0) == 0)
    def _():
        carry[...] = jnp.zeros(carry.shape, F32)

    h = x_ref[...] * mod_ref[4:5, :] + mod_ref[3:4, :]
    h_ref[...] = h
    logit = lax.dot_general(wr_ref[...], h, NT_DIMS, precision=lax.Precision.HIGHEST,
                            preferred_element_type=F32) + br_ref[...]
    eid = lax.broadcasted_iota(jnp.int32, logit.shape, 0)
    tops, hots = [], []
    for k in range(TOP_K):
        m = jnp.max(logit, axis=0, keepdims=True)
        idx = jnp.min(jnp.where(logit == m, eid, ne), axis=0, keepdims=True)
        hot = eid == idx
        logit = jnp.where(hot, -jnp.inf, logit)
        tops.append(m)
        hots.append(hot)
        e_ref[k:k + 1, :] = idx
    ex = [jnp.exp(m - tops[0]) for m in tops]
    den = ex[0] + ex[1] + ex[2] + ex[3]
    for k in range(TOP_K):
        g_ref[k:k + 1, :] = ex[k] / den
    chosen = hots[0] | hots[1] | hots[2] | hots[3]
    onehot = jnp.where(chosen, 1.0, 0.0).astype(BF16)
    earlier = (lax.broadcasted_iota(jnp.int32, (TM, TM), 0)
               < lax.broadcasted_iota(jnp.int32, (TM, TM), 1))
    before = jnp.dot(onehot, jnp.where(earlier, 1.0, 0.0).astype(BF16), preferred_element_type=F32)
    base = before + carry[...]
    for k in range(TOP_K):
        r_ref[k:k + 1, :] = jnp.sum(jnp.where(hots[k], base, 0.0), axis=0, keepdims=True).astype(jnp.int32)
    total = carry[...] + jnp.sum(onehot.astype(F32), axis=1, keepdims=True)
    carry[...] = total
    cnt_ref[...] = jnp.broadcast_to(total, cnt_ref.shape)


def _dispatch_kernel(h_ref, dest_hbm, xs_in, xs_out, dest_smem, idx_sem, row_sem):
    del xs_in
    i = pl.program_id(0)
    n = dest_smem.shape[0]
    load = pltpu.make_async_copy(dest_hbm.at[i], dest_smem, idx_sem)
    load.start()
    load.wait()

    def row_copy(a):
        return pltpu.make_async_copy(h_ref.at[pl.ds(a % TM, 1)], xs_out.at[pl.ds(dest_smem[a], 1)], row_sem)

    def issue(a, c):
        row_copy(a).start()
        return c

    lax.fori_loop(0, n, issue, 0, unroll=8)

    def drain(a, c):
        row_copy(a).wait()
        return c

    lax.fori_loop(0, n, drain, 0, unroll=8)


def _expert_kernel(ff, be_ref, nu_ref, xs_ref, wgu_ref, bgu_ref, wdn_ref, bdn_ref, ys_ref, a_scr):
    del be_ref
    b = pl.program_id(0)

    @pl.when(b < nu_ref[0])
    def _():
        xb = xs_ref[...].astype(BF16)
        chunk = min(256, ff)
        for j in range(ff // chunk):
            c0 = j * chunk
            gate = jnp.dot(xb, wgu_ref[:, c0:c0 + chunk], preferred_element_type=F32) + bgu_ref[:, c0:c0 + chunk]
            up = (jnp.dot(xb, wgu_ref[:, ff + c0:ff + c0 + chunk], preferred_element_type=F32)
                  + bgu_ref[:, ff + c0:ff + c0 + chunk])
            gate = jnp.minimum(gate, SWIGLU_LIMIT)
            up = jnp.clip(up, -SWIGLU_LIMIT, SWIGLU_LIMIT)
            glu = gate * jax.nn.sigmoid(gate * SWIGLU_ALPHA)
            a_scr[:, c0:c0 + chunk] = ((up + 1.0) * glu).astype(BF16)
        ys_ref[...] = jnp.dot(a_scr[...], wdn_ref[...], preferred_element_type=F32) + bdn_ref[...]

    @pl.when(b >= nu_ref[0])
    def _():
        ys_ref[...] = jnp.zeros(ys_ref.shape, F32)


def _combine_kernel(alpha, x_ref, mod_ref, gt_ref, dest_hbm, ys_hbm, lg_ref, lb_ref, o_ref,
                    dest_smem, buf, idx_sem, row_sem):
    i = pl.program_id(0)
    n = dest_smem.shape[0]
    load = pltpu.make_async_copy(dest_hbm.at[i], dest_smem, idx_sem)
    load.start()
    load.wait()

    def row_copy(a):
        return pltpu.make_async_copy(ys_hbm.at[pl.ds(dest_smem[a], 1)],
                                     buf.at[a // TM, pl.ds(a % TM, 1)], row_sem)

    def issue(a, c):
        row_copy(a).start()
        return c

    lax.fori_loop(0, n, issue, 0, unroll=8)

    def drain(a, c):
        row_copy(a).wait()
        return c

    lax.fori_loop(0, n, drain, 0, unroll=8)
    y = gt_ref[:, 0:1] * buf[0]
    for k in range(1, TOP_K):
        y = y + gt_ref[:, k:k + 1] * buf[k]
    o_ref[...] = _resid_ln(alpha, x_ref[...], y, mod_ref[5:6, :], lg_ref[...], lb_ref[...])


def _moe_layer(xs, mod, group, alpha, w_router, b_router, w_gu, b_gu, w_dn, b_dn, lg, lb):
    t, d = xs.shape
    ne = w_router.shape[1]
    ff = w_dn.shape[1]
    nt = t // TM
    n_assign = t * TOP_K
    bm = 512 if n_assign // ne >= 2048 else 128
    n_blocks = -(-n_assign // bm) + ne
    const2 = lambda i: (0, 0)

    h, top_e, gates, rank, cnt = pl.pallas_call(
        _route_kernel,
        grid=(nt,),
        in_specs=[
            pl.BlockSpec((TM, d), lambda i: (i, 0)),
            pl.BlockSpec((None, 6, d), lambda i: (group(i), 0, 0)),
            pl.BlockSpec((ne, d), const2),
            pl.BlockSpec((ne, 1), const2),
        ],
        out_specs=[
            pl.BlockSpec((TM, d), lambda i: (i, 0)),
            pl.BlockSpec((TOP_K, TM), lambda i: (0, i)),
            pl.BlockSpec((TOP_K, TM), lambda i: (0, i)),
            pl.BlockSpec((TOP_K, TM), lambda i: (0, i)),
            pl.BlockSpec((ne, LANES), const2),
        ],
        out_shape=[
            jax.ShapeDtypeStruct((t, d), F32),
            jax.ShapeDtypeStruct((TOP_K, t), jnp.int32),
            jax.ShapeDtypeStruct((TOP_K, t), F32),
            jax.ShapeDtypeStruct((TOP_K, t), jnp.int32),
            jax.ShapeDtypeStruct((ne, LANES), F32),
        ],
        scratch_shapes=[pltpu.VMEM((ne, 1), F32)],
        compiler_params=_params("arbitrary"),
        name="moe_route",
    )(xs, mod, w_router.T, b_router.reshape(ne, 1))

    counts = cnt[:, 0].astype(jnp.int32)
    padded = (counts + bm - 1) // bm * bm
    pad_end = jnp.cumsum(padded)
    pad_start = pad_end - padded
    dest = pad_start[top_e] + rank
    dest_tiles = dest.reshape(TOP_K, nt, TM).transpose(1, 0, 2).reshape(nt, TOP_K * TM)
    block_e = jnp.minimum(
        jnp.searchsorted(pad_end, jnp.arange(n_blocks, dtype=jnp.int32) * bm, side='right'), ne - 1
    ).astype(jnp.int32)
    n_used = (pad_end[-1:] // bm).astype(jnp.int32)

    n_slots = n_blocks * bm
    xs_slots = pl.pallas_call(
        _dispatch_kernel,
        grid=(nt,),
        in_specs=[
            pl.BlockSpec((TM, d), lambda i: (i, 0)),
            pl.BlockSpec(memory_space=pl.ANY),
            pl.BlockSpec(memory_space=pl.ANY),
        ],
        out_specs=pl.BlockSpec(memory_space=pl.ANY),
        out_shape=jax.ShapeDtypeStruct((n_slots, d), F32),
        scratch_shapes=[pltpu.SMEM((TOP_K * TM,), jnp.int32), pltpu.SemaphoreType.DMA,
                        pltpu.SemaphoreType.DMA],
        input_output_aliases={2: 0},
        compiler_params=_params("arbitrary"),
        name="moe_dispatch",
    )(h, dest_tiles, jnp.zeros((n_slots, d), F32))

    last = lambda b, be, nu: jnp.minimum(b, nu[0] - 1)
    ys_slots = pl.pallas_call(
        functools.partial(_expert_kernel, ff),
        grid_spec=pltpu.PrefetchScalarGridSpec(
            num_scalar_prefetch=2,
            grid=(n_blocks,),
            in_specs=[
                pl.BlockSpec((bm, d), lambda b, be, nu: (last(b, be, nu), 0)),
                pl.BlockSpec((None, d, 2 * ff), lambda b, be, nu: (be[b], 0, 0)),
                pl.BlockSpec((None, 1, 2 * ff), lambda b, be, nu: (be[b], 0, 0)),
                pl.BlockSpec((None, ff, d), lambda b, be, nu: (be[b], 0, 0)),
                pl.BlockSpec((None, 1, d), lambda b, be, nu: (be[b], 0, 0)),
            ],
            out_specs=pl.BlockSpec((bm, d), lambda b, be, nu: (b, 0)),
            scratch_shapes=[pltpu.VMEM((bm, ff), BF16)],
        ),
        out_shape=jax.ShapeDtypeStruct((n_slots, d), F32),
        compiler_params=_params("arbitrary"),
        name="moe_experts",
    )(block_e, n_used, xs_slots, w_gu.astype(BF16), b_gu.reshape(ne, 1, 2 * ff),
      w_dn.astype(BF16), b_dn.reshape(ne, 1, d))

    return pl.pallas_call(
        functools.partial(_combine_kernel, alpha),
        grid=(nt,),
        in_specs=[
            pl.BlockSpec((TM, d), lambda i: (i, 0)),
            pl.BlockSpec((None, 6, d), lambda i: (group(i), 0, 0)),
            pl.BlockSpec((TM, TOP_K), lambda i: (i, 0)),
            pl.BlockSpec(memory_space=pl.ANY),
            pl.BlockSpec(memory_space=pl.ANY),
            pl.BlockSpec((1, d), const2),
            pl.BlockSpec((1, d), const2),
        ],
        out_specs=pl.BlockSpec((TM, d), lambda i: (i, 0)),
        out_shape=jax.ShapeDtypeStruct((t, d), F32),
        scratch_shapes=[pltpu.SMEM((TOP_K * TM,), jnp.int32), pltpu.VMEM((TOP_K, TM, d), F32),
                        pltpu.SemaphoreType.DMA, pltpu.SemaphoreType.DMA],
        compiler_params=_params("arbitrary"),
        name="moe_combine",
    )(xs, mod, gates.T, dest_tiles, ys_slots, lg.reshape(1, d), lb.reshape(1, d))


def kernel(x, c, ctx, c_ctx, w_mod, b_mod, ln_g, ln_b, gmlp_w_in, gmlp_ln_g, gmlp_ln_b, gmlp_w_s, gmlp_b_s, gmlp_w_out, diff_w_qkv, diff_lambda, diff_subln_g, diff_w_o, conv_w_in, conv_w, conv_w_out, moe_w_router, moe_b_router, moe_w_gate_up, moe_b_gate_up, moe_w_down, moe_b_down):
    nb, seq, d = x.shape
    ctx_len = ctx.shape[1]
    depth = w_mod.shape[0]
    per = seq + ctx_len
    assert seq % TM == 0 and ctx_len % TM == 0 and nb + 1 <= MOD_ROWS
    ntb, nlt = per // TM, seq // TM
    group = _group_map(ntb, nlt)
    dims = (nb, seq, ctx_len)
    alpha = (2 * depth) ** 0.25

    cond = jnp.concatenate([c_ctx[None, :], c, jnp.zeros((MOD_ROWS - nb - 1, d), F32)], axis=0)
    mods = _modulation(cond, w_mod, b_mod)[:, :nb + 1].reshape(depth, nb + 1, 6, d)

    xs = jnp.concatenate([x, ctx], axis=1).reshape(nb * per, d)
    for i in range(depth):
        kind, j = i % N_MIXERS, i // N_MIXERS
        mod = mods[i]
        if kind == 0:
            xs = _gmlp_layer(xs, mod, group, alpha, gmlp_w_in[j], gmlp_ln_g[j], gmlp_ln_b[j],
                             gmlp_w_s[j], gmlp_b_s[j], gmlp_w_out[j], ln_g[i, 0], ln_b[i, 0])
        elif kind == 1:
            lam_init = 0.8 - 0.6 * float(np.exp(-0.3 * i))
            xs = _attn_layer(xs, mod, group, alpha, dims, lam_init, diff_w_qkv[j], diff_lambda[j],
                             diff_subln_g[j], diff_w_o[j], ln_g[i, 0], ln_b[i, 0])
        else:
            xs = _conv_layer(xs, mod, group, alpha, dims, conv_w_in[j], conv_w[j], conv_w_out[j],
                             ln_g[i, 0], ln_b[i, 0])
        xs = _moe_layer(xs, mod, group, alpha, moe_w_router[i], moe_b_router[i], moe_w_gate_up[i],
                        moe_b_gate_up[i], moe_w_down[i], moe_b_down[i], ln_g[i, 1], ln_b[i, 1])
    return xs.reshape(nb, per, d)[:, :seq]
```
